```python
import jax, jax.numpy as jnp
from jax import lax
import numpy as np

D_MODEL = 1024
BATCH = 16
SEQ = 2048
DEPTH = 1

D_MIX = D_MODEL
ML_HEADS = 4
ML_DH = 128
ML_WIDTH = ML_HEADS * ML_DH
ML_CHUNK = 64
CONV_K = 4
NSA_HEADS = 8
NSA_KV_HEADS = 2
NSA_DH = 64
NSA_HPG = NSA_HEADS // NSA_KV_HEADS
NSA_WIDTH = NSA_HEADS * NSA_DH
NSA_KV_WIDTH = NSA_KV_HEADS * NSA_DH
CMP_BLOCK = 32
CMP_STRIDE = 16
CMP_HIDDEN = 256
SLC_BLOCK = 64
SLC_TOPN = 16
WINDOW = 512
Q_BLOCK = 128
N_BRANCH = 3
D_IN = 4 * ML_WIDTH + 2 * ML_HEADS + NSA_WIDTH + 6 * NSA_KV_WIDTH + NSA_HEADS * N_BRANCH
D_FF = 2816
EPS = 1e-6
NEG = -1e30
FORCE = 1e30

kernel_name = 'hybrid_mlstm_nsa_macaron'


def rmsnorm(x, g):
    xf = x.astype(jnp.float32)
    y = xf * lax.rsqrt(jnp.mean(xf * xf, axis=-1, keepdims=True) + EPS)
    return (y * g.astype(jnp.float32)).astype(x.dtype)


def swiglu(x, w1, w3, w2):
    return (jax.nn.silu(x @ w1) * (x @ w3)) @ w2


def causal_conv(x, w):
    k, c = w.shape
    return lax.conv_general_dilated(x, w[:, None, :].astype(x.dtype), window_strides=(1,),
                                    padding=[(k - 1, 0)],
                                    dimension_numbers=('NWC', 'WIO', 'NWC'),
                                    feature_group_count=c)


def head_rmsnorm(h, g):
    H, D = h.shape[-2:]
    y = h * lax.rsqrt(jnp.mean(h * h, axis=-1, keepdims=True) + EPS)
    return y * g.reshape(H, D).astype(jnp.float32)


def mlstm_chunkwise(q, k, v, i_pre, f_pre):
    B, T, H, D = q.shape
    L = ML_CHUNK
    nc = T // L
    f32 = jnp.float32

    def to_chunks(a):
        return a.astype(f32).reshape(B, nc, L, H, D).transpose(0, 3, 1, 2, 4)

    q = to_chunks(q) * D ** -0.5
    k = to_chunks(k)
    v = to_chunks(v)
    log_i = i_pre.astype(f32).reshape(B, nc, L, H).transpose(0, 3, 1, 2)
    log_f = jax.nn.log_sigmoid(f_pre.astype(f32)).reshape(B, nc, L, H).transpose(0, 3, 1, 2)
    b = jnp.cumsum(log_f, axis=-1)
    g = b[..., -1]
    causal = jnp.tril(jnp.ones((L, L), dtype=bool))
    d_log = jnp.where(causal, b[..., :, None] - b[..., None, :] + log_i[..., None, :], -jnp.inf)

    w_end = g[..., None] - b + log_i
    m_loc = jnp.max(w_end, axis=-1)
    e = jnp.exp(w_end - m_loc[..., None])
    c_loc = jnp.einsum('bhcl,bhcld,bhcle->bhcde', e, v, k)
    n_loc = jnp.einsum('bhcl,bhcle->bhce', e, k)

    def step(carry, xs):
        c, n, m = carry
        cl, nl, ml, gc = xs
        m_new = jnp.maximum(gc + m, ml)
        a = jnp.exp(gc + m - m_new)
        s = jnp.exp(ml - m_new)
        c_new = a[..., None, None] * c + s[..., None, None] * cl
        n_new = a[..., None] * n + s[..., None] * nl
        return (c_new, n_new, m_new), (c, n, m)

    init = (jnp.zeros((B, H, D, D), f32), jnp.zeros((B, H, D), f32), jnp.zeros((B, H), f32))
    xs = (c_loc.transpose(2, 0, 1, 3, 4), n_loc.transpose(2, 0, 1, 3),
          m_loc.transpose(2, 0, 1), g.transpose(2, 0, 1))
    _, (c_prev, n_prev, m_prev) = lax.scan(step, init, xs)
    c_prev = c_prev.transpose(1, 2, 0, 3, 4)
    n_prev = n_prev.transpose(1, 2, 0, 3)
    m_prev = m_prev.transpose(1, 2, 0)

    m_inter = b + m_prev[..., None]
    m_t = jnp.maximum(m_inter, jnp.max(d_log, axis=-1))
    s = jnp.einsum('bhcld,bhcsd->bhcls', q, k) * jnp.exp(d_log - m_t[..., None])
    r = jnp.exp(m_inter - m_t)
    num = jnp.einsum('bhcls,bhcsd->bhcld', s, v) + r[..., None] * jnp.einsum('bhcde,bhcle->bhcld', c_prev, q)
    den = jnp.sum(s, axis=-1) + r * jnp.einsum('bhce,bhcle->bhcl', n_prev, q)
    h = num / jnp.maximum(jnp.abs(den), jnp.exp(-m_t))[..., None]
    return h.transpose(0, 2, 3, 1, 4).reshape(B, T, H, D)


def compress_blocks(x, pe, w1, b1, w2):
    B, T, G, Dh = x.shape
    n_cmp = (T - CMP_BLOCK) // CMP_STRIDE + 1
    idx = jnp.arange(n_cmp)[:, None] * CMP_STRIDE + jnp.arange(CMP_BLOCK)[None, :]
    blk = x[:, idx] + pe[:, None, :]
    flat = blk.transpose(0, 1, 3, 2, 4).reshape(B, n_cmp, G, CMP_BLOCK * Dh)
    return jax.nn.silu(flat @ w1 + b1) @ w2


def nsa_compressed(q, kc, vc):
    T = q.shape[1]
    n_cmp = kc.shape[1]
    s = jnp.einsum('btghd,bngd->bghtn', q, kc.astype(jnp.float32))
    cmp_end = jnp.arange(n_cmp) * CMP_STRIDE + CMP_BLOCK - 1
    valid = cmp_end[None, :] <= jnp.arange(T)[:, None]
    p = jax.nn.softmax(jnp.where(valid, s, NEG), axis=-1)
    p = jnp.where(valid, p, 0.0)
    o = jnp.einsum('bghtn,bngd->btghd', p, vc.astype(jnp.float32))
    return o, p


def select_blocks(p_cmp, T):
    n_cmp = p_cmp.shape[-1]
    nslc = T // SLC_BLOCK
    c0 = jnp.arange(n_cmp)[:, None] * CMP_STRIDE
    s0 = jnp.arange(nslc)[None, :] * SLC_BLOCK
    overlap = jnp.clip(jnp.minimum(c0 + CMP_BLOCK, s0 + SLC_BLOCK) - jnp.maximum(c0, s0), 0, None)
    overlap = overlap.astype(jnp.float32) / CMP_BLOCK
    imp = jnp.einsum('bghtn,ns->bgts', p_cmp, overlap)
    t = jnp.arange(T)[:, None]
    blk = jnp.arange(nslc)[None, :]
    cur = t // SLC_BLOCK
    visible = blk * SLC_BLOCK <= t
    forced = (blk == 0) | (blk == cur) | (blk == cur - 1)
    score = jnp.where(forced, FORCE, jnp.where(visible, imp, NEG))
    _, idx = lax.top_k(score, min(SLC_TOPN, nslc))
    return idx


def nsa_selected(q, k, v, idx):
    B, T, G, HPG, Dh = q.shape
    nslc = T // SLC_BLOCK
    nqb = T // Q_BLOCK
    n_sel = idx.shape[-1]
    kb = k.astype(jnp.float32).reshape(B, nslc, SLC_BLOCK, G, Dh).transpose(0, 3, 1, 2, 4)
    vb = v.astype(jnp.float32).reshape(B, nslc, SLC_BLOCK, G, Dh).transpose(0, 3, 1, 2, 4)
    qb = q.reshape(B, nqb, Q_BLOCK, G, HPG, Dh)
    ib = idx.reshape(B, G, nqb, Q_BLOCK, n_sel).transpose(0, 2, 1, 3, 4)
    t0s = jnp.arange(nqb) * Q_BLOCK

    def per_batch(args):
        q_b, k_b, v_b, i_b = args

        def per_qblock(a):
            qq, ii, t0 = a
            ksel = jax.vmap(lambda kg, ig: kg[ig])(k_b, ii)
            vsel = jax.vmap(lambda vg, ig: vg[ig])(v_b, ii)
            s = jnp.einsum('qghd,gqnld->gqhnl', qq, ksel)
            pos = ii[..., None] * SLC_BLOCK + jnp.arange(SLC_BLOCK)
            tq = t0 + jnp.arange(Q_BLOCK)
            valid = pos <= tq[None, :, None, None]
            s = jnp.where(valid[:, :, None], s, NEG)
            p = jax.nn.softmax(s.reshape(G, Q_BLOCK, HPG, n_sel * SLC_BLOCK), axis=-1).reshape(s.shape)
            return jnp.einsum('gqhnl,gqnld->qghd', p, vsel)

        return lax.map(per_qblock, (q_b, i_b, t0s))

    o = lax.map(per_batch, (qb, kb, vb, ib))
    return o.reshape(B, T, G, HPG, Dh)


def nsa_window(q, k, v):
    B, T, G, HPG, Dh = q.shape
    nqb = T // Q_BLOCK
    span = WINDOW + Q_BLOCK
    kp = jnp.pad(k.astype(jnp.float32), ((0, 0), (WINDOW, 0), (0, 0), (0, 0)))
    vp = jnp.pad(v.astype(jnp.float32), ((0, 0), (WINDOW, 0), (0, 0), (0, 0)))
    qb = q.reshape(B, nqb, Q_BLOCK, G, HPG, Dh).swapaxes(0, 1)
    t0s = jnp.arange(nqb) * Q_BLOCK

    def per_qblock(a):
        qq, t0 = a
        kk = lax.dynamic_slice_in_dim(kp, t0, span, axis=1)
        vv = lax.dynamic_slice_in_dim(vp, t0, span, axis=1)
        s = jnp.einsum('bqghd,bkgd->bghqk', qq, kk)
        kpos = t0 - WINDOW + jnp.arange(span)
        tq = t0 + jnp.arange(Q_BLOCK)
        valid = (kpos[None, :] <= tq[:, None]) & (kpos[None, :] > tq[:, None] - WINDOW) & (kpos[None, :] >= 0)
        p = jax.nn.softmax(jnp.where(valid, s, NEG), axis=-1)
        return jnp.einsum('bghqk,bkgd->bqghd', p, vv)

    o = lax.map(per_qblock, (qb, t0s))
    return o.swapaxes(0, 1).reshape(B, T, G, HPG, Dh)


def native_sparse_attention(q, kc_tok, vc_tok, k_slc, v_slc, k_win, v_win, gate_pre,
                            k_pe, k_w1, k_b1, k_w2, v_pe, v_w1, v_b1, v_w2):
    B, T, _ = q.shape
    G, HPG, Dh = NSA_KV_HEADS, NSA_HPG, NSA_DH
    qf = q.astype(jnp.float32).reshape(B, T, G, HPG, Dh) * Dh ** -0.5

    def kv(a):
        return a.reshape(B, T, G, Dh)

    kc = compress_blocks(kv(kc_tok), k_pe, k_w1, k_b1, k_w2)
    vc = compress_blocks(kv(vc_tok), v_pe, v_w1, v_b1, v_w2)
    o_cmp, p_cmp = nsa_compressed(qf, kc, vc)
    idx = select_blocks(p_cmp, T)
    o_slc = nsa_selected(qf, kv(k_slc), kv(v_slc), idx)
    o_win = nsa_window(qf, kv(k_win), kv(v_win))
    gates = jax.nn.sigmoid(gate_pre.astype(jnp.float32)).reshape(B, T, G, HPG, N_BRANCH)
    o = gates[..., 0:1] * o_cmp + gates[..., 1:2] * o_slc + gates[..., 2:3] * o_win
    return o.reshape(B, T, NSA_WIDTH).astype(q.dtype)


def hybrid_mixer(h, w_in, conv_w, ml_b_i, ml_b_f, ml_gn,
                 k_pe, k_w1, k_b1, k_w2, v_pe, v_w1, v_b1, v_w2, w_out):
    B, T, _ = h.shape
    proj = h @ w_in
    sizes = [ML_WIDTH] * 4 + [ML_HEADS] * 2 + [NSA_WIDTH] + [NSA_KV_WIDTH] * 6 + [NSA_HEADS * N_BRANCH]
    offs = np.cumsum(sizes)[:-1].tolist()
    mq, mk, mv, mo, mi, mf, nq, kc, vc, ks, vs, kw, vw, ng = jnp.split(proj, offs, axis=-1)

    qk = jax.nn.silu(causal_conv(jnp.concatenate([mq, mk], axis=-1), conv_w))
    mq, mk = jnp.split(qk, 2, axis=-1)

    def heads(a):
        return a.reshape(B, T, ML_HEADS, ML_DH)

    hm = mlstm_chunkwise(heads(mq), heads(mk), heads(mv), mi + ml_b_i, mf + ml_b_f)
    hm = head_rmsnorm(hm, ml_gn) * jax.nn.sigmoid(heads(mo).astype(jnp.float32))
    hm = hm.reshape(B, T, ML_WIDTH).astype(h.dtype)

    hn = native_sparse_attention(nq, kc, vc, ks, vs, kw, vw, ng,
                                 k_pe, k_w1, k_b1, k_w2, v_pe, v_w1, v_b1, v_w2)
    return jnp.concatenate([hm, hn], axis=-1) @ w_out


def setup_inputs(seed: int = 0) -> dict:
    key = jax.random.key(seed)
    ks = jax.random.split(key, 32)
    f32 = jnp.float32

    def nrm(k, shape, fan_in):
        return jax.random.normal(k, shape, f32) * fan_in ** -0.5

    def gain(k, shape):
        return 1.0 + 0.02 * jax.random.normal(k, shape, f32)

    def small(k, shape, scale):
        return scale * jax.random.normal(k, shape, f32)

    L = DEPTH
    fb = jnp.linspace(3.0, 6.0, ML_HEADS, dtype=f32)[None, :] + small(ks[9], (L, ML_HEADS), 0.1)
    return {
        'x': jax.random.normal(ks[0], (BATCH, SEQ, D_MODEL), f32),
        'ffn1_norm': gain(ks[1], (L, D_MODEL)),
        'ffn1_w1': nrm(ks[2], (L, D_MODEL, D_FF), D_MODEL),
        'ffn1_w3': nrm(ks[3], (L, D_MODEL, D_FF), D_MODEL),
        'ffn1_w2': nrm(ks[4], (L, D_FF, D_MODEL), D_FF),
        'mix_norm': gain(ks[5], (L, D_MODEL)),
        'w_in': nrm(ks[6], (L, D_MODEL, D_IN), D_MODEL),
        'conv_w': nrm(ks[7], (L, CONV_K, 2 * ML_WIDTH), CONV_K),
        'ml_b_i': small(ks[8], (L, ML_HEADS), 0.1),
        'ml_b_f': fb,
        'ml_gn': gain(ks[10], (L, ML_WIDTH)),
        'cmp_k_pe': small(ks[11], (L, CMP_BLOCK, NSA_DH), 0.02),
        'cmp_k_w1': nrm(ks[12], (L, CMP_BLOCK * NSA_DH, CMP_HIDDEN), CMP_BLOCK * NSA_DH),
        'cmp_k_b1': small(ks[13], (L, CMP_HIDDEN), 0.02),
        'cmp_k_w2': nrm(ks[14], (L, CMP_HIDDEN, NSA_DH), CMP_HIDDEN),
        'cmp_v_pe': small(ks[15], (L, CMP_BLOCK, NSA_DH), 0.02),
        'cmp_v_w1': nrm(ks[16], (L, CMP_BLOCK * NSA_DH, CMP_HIDDEN), CMP_BLOCK * NSA_DH),
        'cmp_v_b1': small(ks[17], (L, CMP_HIDDEN), 0.02),
        'cmp_v_w2': nrm(ks[18], (L, CMP_HIDDEN, NSA_DH), CMP_HIDDEN),
        'w_out': nrm(ks[19], (L, D_MIX, D_MODEL), D_MIX),
        'ffn2_norm': gain(ks[20], (L, D_MODEL)),
        'ffn2_w1': nrm(ks[21], (L, D_MODEL, D_FF), D_MODEL),
        'ffn2_w3': nrm(ks[22], (L, D_MODEL, D_FF), D_MODEL),
        'ffn2_w2': nrm(ks[23], (L, D_FF, D_MODEL), D_FF),
        'final_norm': gain(ks[24], (D_MODEL,)),
    }


def reference(x, ffn1_norm, ffn1_w1, ffn1_w3, ffn1_w2, mix_norm, w_in, conv_w, ml_b_i, ml_b_f,
              ml_gn, cmp_k_pe, cmp_k_w1, cmp_k_b1, cmp_k_w2, cmp_v_pe, cmp_v_w1, cmp_v_b1,
              cmp_v_w2, w_out, ffn2_norm, ffn2_w1, ffn2_w3, ffn2_w2, final_norm):
    for l in range(DEPTH):
        x = x + 0.5 * swiglu(rmsnorm(x, ffn1_norm[l]), ffn1_w1[l], ffn1_w3[l], ffn1_w2[l])
        h = rmsnorm(x, mix_norm[l])
        x = x + hybrid_mixer(h, w_in[l], conv_w[l], ml_b_i[l], ml_b_f[l], ml_gn[l],
                             cmp_k_pe[l], cmp_k_w1[l], cmp_k_b1[l], cmp_k_w2[l],
                             cmp_v_pe[l], cmp_v_w1[l], cmp_v_b1[l], cmp_v_w2[l], w_out[l])
        x = x + 0.5 * swiglu(rmsnorm(x, ffn2_norm[l]), ffn2_w1[l], ffn2_w3[l], ffn2_w2[l])
    return rmsnorm(x, final_norm)
```

```python
import functools

import jax
import jax.numpy as jnp
import numpy as np
from jax import lax
from jax.experimental import pallas as pl
from jax.experimental.pallas import tpu as pltpu

F32 = jnp.float32
BF16 = jnp.bfloat16

EPS = 1e-6
NEG = -1e30
FORCE = 1e30
MASK = -(2.0 ** 100)

ML_HEADS = 4
ML_DH = 128
ML_WIDTH = ML_HEADS * ML_DH
ML_CHUNK = 128
CONV_K = 4
NSA_G = 2
NSA_HPG = 4
NSA_DH = 64
NSA_WIDTH = NSA_G * NSA_HPG * NSA_DH
NSA_KV = NSA_G * NSA_DH
CMP_BLOCK = 32
CMP_STRIDE = 16
SLC_BLOCK = 64
SLC_TOPN = 16
WINDOW = 512
N_BRANCH = 3
GATE_ROWS = 16

V7X_VMEM_BYTES = 64 * 1024 * 1024
VMEM_LIMIT = 56 * 1024 * 1024

NT = (((1,), (1,)), ((), ()))


def _dot(a, b):
    return jnp.dot(a, b, preferred_element_type=F32)


def _dot_nt(a, b):
    return lax.dot_general(a, b, NT, preferred_element_type=F32)


def _sigmoid(x):
    return 1.0 / (1.0 + jnp.exp(-x))


def _silu(x):
    return x * _sigmoid(x)


def _log_sigmoid(x):
    return -(jnp.maximum(-x, 0.0) + jnp.log(1.0 + jnp.exp(-jnp.abs(x))))


def _rms(x, g):
    return x * lax.rsqrt(jnp.mean(x * x, axis=-1, keepdims=True) + EPS) * g


def _split3(x):
    hi = x.astype(BF16)
    r = x - hi.astype(F32)
    mid = r.astype(BF16)
    lo = (r - mid.astype(F32)).astype(BF16)
    return hi, mid, lo


def _resident(shape):
    nd = len(shape)
    return pl.BlockSpec(shape, lambda *_: (0,) * nd, pipeline_mode=pl.Buffered(1))


def _params(sem):
    return pltpu.CompilerParams(dimension_semantics=sem, vmem_limit_bytes=VMEM_LIMIT)


FF_CHUNK = 256


def _swiglu(h, w1_ref, w3_ref, w2_ref):
    d_ff = w1_ref.shape[1]
    acc = None
    for c in range(d_ff // FF_CHUNK):
        sl = slice(c * FF_CHUNK, (c + 1) * FF_CHUNK)
        a = _dot(h, w1_ref[:, sl])
        b = _dot(h, w3_ref[:, sl])
        g = (_silu(a) * b).astype(BF16)
        y = _dot(g, w2_ref[sl, :])
        acc = y if acc is None else acc + y
    return acc


def _ffn1_kernel(x_ref, g_ref, w1_ref, w3_ref, w2_ref, gm_ref, x1_ref, h_ref):
    x = x_ref[...]
    h = _rms(x, g_ref[...]).astype(BF16)
    x1 = x + 0.5 * _swiglu(h, w1_ref, w3_ref, w2_ref)
    x1_ref[...] = x1
    h_ref[...] = _rms(x1, gm_ref[...]).astype(BF16)


def _ffn1(x2d, g1, w1, w3, w2, gmix, tm=512):
    n, d = x2d.shape
    f = w1.shape[1]
    row = pl.BlockSpec((tm, d), lambda i: (i, 0))
    return pl.pallas_call(
        _ffn1_kernel,
        grid=(n // tm,),
        in_specs=[row, _resident((1, d)), _resident((d, f)), _resident((d, f)),
                  _resident((f, d)), _resident((1, d))],
        out_specs=[row, row],
        out_shape=[jax.ShapeDtypeStruct((n, d), F32), jax.ShapeDtypeStruct((n, d), BF16)],
        compiler_params=_params(("parallel",)),
        name="ffn1",
    )(x2d, g1, w1, w3, w2, gmix)


def _out_ffn2_kernel(x1_ref, hm_ref, hnt_ref, wom_ref, won_ref, g_ref, w1_ref, w3_ref, w2_ref,
                     gf_ref, o_ref, *, final):
    hm = hm_ref[0].astype(BF16)
    hn = hnt_ref[0].T.astype(BF16)
    x2 = x1_ref[0] + _dot(hm, wom_ref[...]) + _dot(hn, won_ref[...])
    h = _rms(x2, g_ref[...]).astype(BF16)
    x3 = x2 + 0.5 * _swiglu(h, w1_ref, w3_ref, w2_ref)
    o_ref[0] = _rms(x3, gf_ref[...]) if final else x3


def _out_ffn2(x1, hm, hnt, wo_m, wo_n, g2, w1, w3, w2, gfin, final, tm=512):
    b, t, d = x1.shape
    f = w1.shape[1]
    wm = hm.shape[2]
    wn = hnt.shape[1]
    return pl.pallas_call(
        functools.partial(_out_ffn2_kernel, final=final),
        grid=(b, t // tm),
        in_specs=[pl.BlockSpec((1, tm, d), lambda i, j: (i, j, 0)),
                  pl.BlockSpec((1, tm, wm), lambda i, j: (i, j, 0)),
                  pl.BlockSpec((1, wn, tm), lambda i, j: (i, 0, j)),
                  _resident((wm, d)), _resident((wn, d)), _resident((1, d)),
                  _resident((d, f)), _resident((d, f)), _resident((f, d)), _resident((1, d))],
        out_specs=pl.BlockSpec((1, tm, d), lambda i, j: (i, j, 0)),
        out_shape=jax.ShapeDtypeStruct((b, t, d), F32),
        compiler_params=_params(("parallel", "parallel")),
        name="out_ffn2",
    )(x1, hm, hnt, wo_m, wo_n, g2, w1, w3, w2, gfin)


TOK_SLABS = (("qk", 2 * ML_WIDTH), ("v", ML_WIDTH), ("o", ML_WIDTH), ("kc", NSA_KV),
             ("vc", NSA_KV), ("ks", NSA_KV), ("kw", NSA_KV), ("g", 128))
FM_SLABS = (("nq", NSA_WIDTH), ("vsw", 2 * NSA_KV), ("mv", ML_WIDTH), ("mg", GATE_ROWS),
            ("ng", NSA_G * GATE_ROWS))


def _in_proj_kernel(h_ref, wt_ref, wf_ref, *out_refs):
    h = h_ref[0]
    n_tok = len(TOK_SLABS)
    off = 0
    for (_, w), ref in zip(TOK_SLABS, out_refs[:n_tok]):
        ref[0] = _dot(h, wt_ref[:, off:off + w])
        off += w
    off = 0
    for (_, r), ref in zip(FM_SLABS, out_refs[n_tok:]):
        ref[0] = _dot_nt(wf_ref[off:off + r, :], h)
        off += r


def _in_proj(h3, wt, wf, tm=512):
    b, t, d = h3.shape
    out_specs, out_shape = [], []
    for _, w in TOK_SLABS:
        out_specs.append(pl.BlockSpec((1, tm, w), lambda i, j: (i, j, 0)))
        out_shape.append(jax.ShapeDtypeStruct((b, t, w), F32))
    for _, r in FM_SLABS:
        out_specs.append(pl.BlockSpec((1, r, tm), lambda i, j: (i, 0, j)))
        out_shape.append(jax.ShapeDtypeStruct((b, r, t), F32))
    outs = pl.pallas_call(
        _in_proj_kernel,
        grid=(b, t // tm),
        in_specs=[pl.BlockSpec((1, tm, d), lambda i, j: (i, j, 0)),
                  _resident(wt.shape), _resident(wf.shape)],
        out_specs=out_specs,
        out_shape=out_shape,
        compiler_params=_params(("parallel", "parallel")),
        name="in_proj",
    )(h3, wt, wf)
    names = [n for n, _ in TOK_SLABS] + [n + "T" for n, _ in FM_SLABS]
    return dict(zip(names, outs))


def _in_proj_weights(w_in):
    sizes = [ML_WIDTH] * 4 + [ML_HEADS] * 2 + [NSA_WIDTH] + [NSA_KV] * 6 + [NSA_G * NSA_HPG * N_BRANCH]
    offs = np.cumsum([0] + sizes)
    mq, mk, mv, mo, mi, mf, nq, kc, vc, ks, vs, kw, vw, ng = (
        w_in[:, offs[i]:offs[i + 1]] for i in range(len(sizes)))
    d = w_in.shape[0]
    gpad = jnp.zeros((d, 128 - 2 * ML_HEADS), w_in.dtype)
    wt = jnp.concatenate([mq, mk, mv, mo, kc, vc, ks, kw, mi, mf, gpad], axis=1)
    per_g = NSA_HPG * N_BRANCH
    zg = jnp.zeros((d, GATE_ROWS - per_g), w_in.dtype)
    ng_rows = jnp.concatenate([ng[:, :per_g], zg, ng[:, per_g:], zg], axis=1)
    mg_rows = jnp.concatenate([mi, mf, jnp.zeros((d, GATE_ROWS - 2 * ML_HEADS), w_in.dtype)], axis=1)
    wf = jnp.concatenate([nq, vs, vw, mv, mg_rows, ng_rows], axis=1).T
    return wt.astype(BF16), wf.astype(BF16)


def _causal_conv_silu(x, w):
    t = x.shape[0]
    rows = lax.broadcasted_iota(jnp.int32, x.shape, 0)
    y = x * w[CONV_K - 1:CONV_K, :]
    for j in range(CONV_K - 1):
        s = CONV_K - 1 - j
        xs = jnp.where(rows >= s, pltpu.roll(x, s, axis=0), 0.0)
        y = y + xs * w[j:j + 1, :]
    del t
    return _silu(y)


def _mlstm_kernel(q_ref, k_ref, v_ref, vt_ref, o_ref, gtok_ref, grow_ref, wq_ref, wk_ref,
                  blane_ref, bcol_ref, gn_ref, out_ref, q_scr, k_scr):
    L = ML_CHUNK
    hd = pl.program_id(1)
    t = q_ref.shape[1]
    q_scr[...] = _causal_conv_silu(q_ref[0], wq_ref[...]) * (ML_DH ** -0.5)
    k_scr[...] = _causal_conv_silu(k_ref[0], wk_ref[...])

    r_i = lax.broadcasted_iota(jnp.int32, (L, L), 0)
    c_i = lax.broadcasted_iota(jnp.int32, (L, L), 1)
    causal = r_i >= c_i
    tril = jnp.where(causal, 1.0, 0.0).astype(BF16)
    triu = jnp.where(r_i <= c_i, 1.0, 0.0).astype(BF16)
    lane = lax.broadcasted_iota(jnp.int32, (L, 128), 1)
    srow = lax.broadcasted_iota(jnp.int32, (GATE_ROWS, L), 0)
    gn = gn_ref[...]

    c_state = jnp.zeros((ML_DH, ML_DH), F32)
    n_state = jnp.zeros((1, ML_DH), F32)
    m_state = jnp.zeros((1, 1), F32)

    for c in range(t // L):
        sl = slice(c * L, (c + 1) * L)
        qc = q_scr[sl, :]
        kc = k_scr[sl, :]
        qb = qc.astype(BF16)
        kb = kc.astype(BF16)

        gt = gtok_ref[0, sl, :] + blane_ref[...]
        lf_t = jnp.where((lane >= ML_HEADS) & (lane < 2 * ML_HEADS), _log_sigmoid(gt), 0.0)
        hi, mid, lo = _split3(lf_t)
        cs_t = _dot(tril, hi) + _dot(tril, mid) + _dot(tril, lo)
        b_col = jnp.sum(jnp.where(lane == hd + ML_HEADS, cs_t, 0.0), axis=1, keepdims=True)
        i_col = jnp.sum(jnp.where(lane == hd, gt, 0.0), axis=1, keepdims=True)

        gr = grow_ref[0, :, sl] + bcol_ref[...]
        lf_r = jnp.where((srow >= ML_HEADS) & (srow < 2 * ML_HEADS), _log_sigmoid(gr), 0.0)
        hi, mid, lo = _split3(lf_r)
        cs_r = _dot(hi, triu) + _dot(mid, triu) + _dot(lo, triu)
        b_row = jnp.sum(jnp.where(srow == hd + ML_HEADS, cs_r, 0.0), axis=0, keepdims=True)
        i_row = jnp.sum(jnp.where(srow == hd, gr, 0.0), axis=0, keepdims=True)
        g_tot = b_row[:, L - 1:L]

        d_log = b_col - b_row + i_row
        m_intra = jnp.max(jnp.where(causal, d_log, -jnp.inf), axis=1, keepdims=True)
        m_inter = b_col + m_state
        m_t = jnp.maximum(m_inter, m_intra)
        decay = jnp.where(causal, jnp.exp(d_log - m_t), 0.0)
        s_mat = _dot_nt(qb, kb) * decay
        r = jnp.exp(m_inter - m_t)
        inter = _dot_nt(qb, c_state.astype(BF16))
        num = _dot(s_mat.astype(BF16), v_ref[0, sl, :].astype(BF16)) + r * inter
        den = (jnp.sum(s_mat, axis=1, keepdims=True)
               + r * jnp.sum(qc * n_state, axis=1, keepdims=True))
        hh = num / jnp.maximum(jnp.abs(den), jnp.exp(-m_t))
        y = hh * lax.rsqrt(jnp.mean(hh * hh, axis=1, keepdims=True) + EPS) * gn
        out_ref[0, sl, :] = y * _sigmoid(o_ref[0, sl, :])

        w_row = g_tot - b_row + i_row
        m_loc = jnp.max(w_row, axis=1, keepdims=True)
        m_new = jnp.maximum(g_tot + m_state, m_loc)
        a = jnp.exp(g_tot + m_state - m_new)
        e_row = jnp.exp(w_row - m_new)
        e_col = jnp.exp(g_tot - b_col + i_col - m_new)
        c_state = a * c_state + _dot((vt_ref[0, :, sl] * e_row).astype(BF16), kb)
        n_state = a * n_state + jnp.sum(e_col * kc, axis=0, keepdims=True)
        m_state = m_new


def _mlstm(p, conv_w, b_lane, b_col, gn):
    qk, v, vt, o, gtok, grow = p["qk"], p["v"], p["mvT"], p["o"], p["g"], p["mgT"]
    b, t, _ = qk.shape
    hw = ML_DH
    tok = lambda off: pl.BlockSpec((1, t, hw), lambda i, h: (i, 0, h + off))
    return pl.pallas_call(
        _mlstm_kernel,
        grid=(b, ML_HEADS),
        in_specs=[tok(0), tok(ML_HEADS), tok(0),
                  pl.BlockSpec((1, hw, t), lambda i, h: (i, h, 0)),
                  tok(0),
                  pl.BlockSpec((1, t, 128), lambda i, h: (i, 0, 0)),
                  pl.BlockSpec((1, GATE_ROWS, t), lambda i, h: (i, 0, 0)),
                  pl.BlockSpec((CONV_K, hw), lambda i, h: (0, h)),
                  pl.BlockSpec((CONV_K, hw), lambda i, h: (0, h + ML_HEADS)),
                  pl.BlockSpec((1, 128), lambda i, h: (0, 0)),
                  pl.BlockSpec((GATE_ROWS, 1), lambda i, h: (0, 0)),
                  pl.BlockSpec((1, hw), lambda i, h: (0, h))],
        out_specs=tok(0),
        out_shape=jax.ShapeDtypeStruct((b, t, ML_WIDTH), F32),
        scratch_shapes=[pltpu.VMEM((t, hw), F32), pltpu.VMEM((t, hw), F32)],
        compiler_params=_params(("parallel", "parallel")),
        name="mlstm",
    )(qk, qk, v, vt, o, gtok, grow, conv_w, conv_w, b_lane, b_col, gn)


def _compress_kernel(xk_ref, xv_ref, kw0_ref, kw1_ref, kpe_ref, kw1f_ref, kb1_ref, kw2_ref,
                     vw0_ref, vw1_ref, vpe_ref, vw1f_ref, vb1_ref, vw2t_ref, kc_ref, vct_ref):
    hid = kw2_ref.shape[0]

    def hidden(x_ref, w0_ref, w1_ref, pe_ref, w1f_ref, b1_ref):
        x = x_ref[0].astype(BF16)
        n = x.shape[0]
        p0 = _dot(x, w0_ref[...])
        p1 = _dot(x, w1_ref[...])
        pe = jnp.broadcast_to(pe_ref[...], (8, pe_ref.shape[1])).astype(BF16)
        cvec = _dot(pe, w1f_ref[...])[0:1, :] + b1_ref[...]
        cvec = jnp.concatenate([cvec] * NSA_G, axis=1)
        return _silu(p0 + pltpu.roll(p1, n - 1, axis=0) + cvec)

    hk = hidden(xk_ref, kw0_ref, kw1_ref, kpe_ref, kw1f_ref, kb1_ref)
    hv = hidden(xv_ref, vw0_ref, vw1_ref, vpe_ref, vw1f_ref, vb1_ref)
    for g in range(NSA_G):
        sl = slice(g * hid, (g + 1) * hid)
        kc_ref[0, g] = _dot(hk[:, sl].astype(BF16), kw2_ref[...])
        vct_ref[0, g] = _dot_nt(vw2t_ref[...], hv[:, sl].astype(BF16))


def _compress_weights(w1):
    hid = w1.shape[1]
    w = w1.reshape(2, CMP_STRIDE, NSA_DH, hid)
    z = jnp.zeros_like(w)
    g0 = jnp.concatenate([w, z], axis=-1)
    g1 = jnp.concatenate([z, w], axis=-1)
    big = jnp.stack([g0, g1], axis=2)
    big = big.reshape(2, CMP_STRIDE * NSA_G * NSA_DH, 2 * hid).astype(BF16)
    return big[0], big[1]


def _compress(kc_tok, vc_tok, k_pe, k_w1, k_b1, k_w2, v_pe, v_w1, v_b1, v_w2):
    b, t, _ = kc_tok.shape
    n_slab = t // CMP_STRIDE
    hid = k_w1.shape[1]
    xk = kc_tok.reshape(b, n_slab, CMP_STRIDE * NSA_KV)
    xv = vc_tok.reshape(b, n_slab, CMP_STRIDE * NSA_KV)
    kw0, kw1 = _compress_weights(k_w1)
    vw0, vw1 = _compress_weights(v_w1)
    xspec = pl.BlockSpec((1, n_slab, CMP_STRIDE * NSA_KV), lambda i: (i, 0, 0))
    args = (xk, xv,
            kw0, kw1, k_pe.reshape(1, -1), k_w1.astype(BF16), k_b1.reshape(1, -1), k_w2.astype(BF16),
            vw0, vw1, v_pe.reshape(1, -1), v_w1.astype(BF16), v_b1.reshape(1, -1),
            v_w2.T.astype(BF16))
    return pl.pallas_call(
        _compress_kernel,
        grid=(b,),
        in_specs=[xspec, xspec] + [_resident(a.shape) for a in args[2:]],
        out_specs=[pl.BlockSpec((1, NSA_G, n_slab, NSA_DH), lambda i: (i, 0, 0, 0)),
                   pl.BlockSpec((1, NSA_G, NSA_DH, n_slab), lambda i: (i, 0, 0, 0))],
        out_shape=[jax.ShapeDtypeStruct((b, NSA_G, n_slab, NSA_DH), F32),
                   jax.ShapeDtypeStruct((b, NSA_G, NSA_DH, n_slab), F32)],
        compiler_params=_params(("parallel",)),
        name="compress",
    )(*args)


def _cmp_attn_kernel(qt_ref, kc_ref, vct_ref, o_ref, bias_ref):
    tq = qt_ref.shape[2]
    n_cmp = kc_ref.shape[2]
    n_slc = bias_ref.shape[2]
    t0 = pl.program_id(2) * tq
    kcm = kc_ref[0, 0].astype(BF16)
    vct = vct_ref[0, 0].astype(BF16)
    n_i = lax.broadcasted_iota(jnp.int32, (n_cmp, tq), 0)
    t_i = lax.broadcasted_iota(jnp.int32, (n_cmp, tq), 1) + t0
    valid = n_i * CMP_STRIDE + (CMP_BLOCK - 1) <= t_i
    psum = jnp.zeros((n_cmp, tq), F32)
    for h in range(NSA_HPG):
        sl = slice(h * NSA_DH, (h + 1) * NSA_DH)
        qh = (qt_ref[0, sl, :] * (NSA_DH ** -0.5)).astype(BF16)
        s = jnp.where(valid, _dot(kcm, qh), NEG)
        m = jnp.max(s, axis=0, keepdims=True)
        e = jnp.exp(s - m)
        p = jnp.where(valid, e * (1.0 / jnp.sum(e, axis=0, keepdims=True)), 0.0)
        o_ref[0, sl, :] = _dot(vct, p.astype(BF16))
        psum = psum + p

    j_o = lax.broadcasted_iota(jnp.int32, (n_slc, n_cmp), 0) * SLC_BLOCK
    c_o = lax.broadcasted_iota(jnp.int32, (n_slc, n_cmp), 1) * CMP_STRIDE
    ov = jnp.maximum(jnp.minimum(c_o + CMP_BLOCK, j_o + SLC_BLOCK) - jnp.maximum(c_o, j_o), 0)
    ov = (ov.astype(F32) * (1.0 / CMP_BLOCK)).astype(BF16)
    hi, mid, lo = _split3(psum)
    imp = _dot(ov, hi) + _dot(ov, mid) + _dot(ov, lo)

    j_i = lax.broadcasted_iota(jnp.int32, (n_slc, tq), 0)
    tt = lax.broadcasted_iota(jnp.int32, (n_slc, tq), 1) + t0
    cur = lax.shift_right_logical(tt, int(np.log2(SLC_BLOCK)))
    forced = (j_i == 0) | (j_i == cur) | (j_i == cur - 1)
    score = jnp.where(forced, FORCE, jnp.where(j_i * SLC_BLOCK <= tt, imp, NEG))
    rank = jnp.zeros((n_slc, tq), F32)
    for i in range(n_slc):
        row = score[i:i + 1, :]
        beats = (row > score) | ((row == score) & (j_i > i))
        rank = rank + jnp.where(beats, 1.0, 0.0)
    n_sel = min(SLC_TOPN, n_slc)
    bias_ref[0, 0] = jnp.where(rank < n_sel, 0.0, MASK).astype(BF16)


def _cmp_attn(nqt, kc, vct, tq=512):
    b, _, t = nqt.shape
    n_cmp = kc.shape[2]
    n_slc = t // SLC_BLOCK
    gw = NSA_HPG * NSA_DH
    return pl.pallas_call(
        _cmp_attn_kernel,
        grid=(b, NSA_G, t // tq),
        in_specs=[pl.BlockSpec((1, gw, tq), lambda i, g, j: (i, g, j)),
                  pl.BlockSpec((1, 1, n_cmp, NSA_DH), lambda i, g, j: (i, g, 0, 0)),
                  pl.BlockSpec((1, 1, NSA_DH, n_cmp), lambda i, g, j: (i, g, 0, 0))],
        out_specs=[pl.BlockSpec((1, gw, tq), lambda i, g, j: (i, g, j)),
                   pl.BlockSpec((1, 1, n_slc, tq), lambda i, g, j: (i, g, 0, j))],
        out_shape=[jax.ShapeDtypeStruct((b, NSA_WIDTH, t), F32),
                   jax.ShapeDtypeStruct((b, NSA_G, n_slc, t), BF16)],
        compiler_params=_params(("parallel", "parallel", "parallel")),
        name="cmp_attn",
    )(nqt, kc, vct)


SP_TQ = 128
SP_KC = 256
AUG = 128


def _sparse_kernel(qt_ref, bias_ref, ks_ref, kw_ref, vst_ref, vwt_ref, oc_ref, ng_ref, out_ref,
                   ks_scr, kw_scr, vs_scr, vw_scr):
    g = pl.program_id(1)
    i = pl.program_id(2)
    t = ks_ref.shape[1]
    n_slc = bias_ref.shape[2]
    tq = SP_TQ
    cols = NSA_HPG * tq

    @pl.when(i == 0)
    def _():
        lane = lax.broadcasted_iota(jnp.int32, (t, AUG), 1)
        key = lax.broadcasted_iota(jnp.int32, (t, AUG), 0)
        ind = jnp.where(lane - NSA_DH == lax.shift_right_logical(key, int(np.log2(SLC_BLOCK))),
                        1.0, 0.0)

        def group_lanes(x):
            return jnp.where(g == 0, x, pltpu.roll(x, NSA_DH, axis=1))

        ks_scr[...] = jnp.where(lane < NSA_DH, group_lanes(ks_ref[0]), ind).astype(BF16)
        kw_scr[...] = jnp.where(lane < NSA_DH, group_lanes(kw_ref[0]), 0.0).astype(BF16)
        ones_row = jnp.where(lax.broadcasted_iota(jnp.int32, (AUG - NSA_DH, SP_KC), 0) == 0, 1.0, 0.0)
        for c in range(t // SP_KC):
            sl = slice(c * SP_KC, (c + 1) * SP_KC)
            vs_scr[c] = jnp.concatenate([vst_ref[0, :, sl], ones_row], axis=0).astype(BF16)
        for c in range(t // tq):
            sl = slice(c * tq, (c + 1) * tq)
            vw_scr[c] = jnp.concatenate([vwt_ref[0, :, sl], ones_row[:, :tq]], axis=0).astype(BF16)

    q_rows = jnp.concatenate(
        [(qt_ref[0, h * NSA_DH:(h + 1) * NSA_DH, :] * (NSA_DH ** -0.5)).astype(BF16)
         for h in range(NSA_HPG)], axis=1)
    b_rows = jnp.concatenate([bias_ref[0, 0]] * NSA_HPG, axis=1)
    z_rows = jnp.zeros((AUG - NSA_DH - n_slc, cols), BF16)
    q_aug = jnp.concatenate([q_rows, b_rows, z_rows], axis=0)

    q_pos = i * tq + (lax.broadcasted_iota(jnp.int32, (1, cols), 1) & (tq - 1))

    def step(k_blk, v_blk, keep, carry):
        m, acc = carry
        s = _dot(k_blk, q_aug)
        if keep is not None:
            s = jnp.where(keep, s, MASK)
        m_new = jnp.maximum(m, jnp.max(s, axis=0, keepdims=True))
        p = jnp.exp(s - m_new).astype(BF16)
        acc = jnp.exp(m - m_new) * acc + _dot(v_blk, p)
        return m_new, acc

    init = (jnp.full((1, cols), MASK, F32), jnp.zeros((AUG, cols), F32))

    def sel_full(c, carry):
        off = pl.multiple_of(c * SP_KC, SP_KC)
        return step(ks_scr[pl.ds(off, SP_KC), :], vs_scr[c], None, carry)

    c_diag = i // (SP_KC // tq)
    carry = lax.fori_loop(0, c_diag, sel_full, init)
    off = pl.multiple_of(c_diag * SP_KC, SP_KC)
    k_pos = off + lax.broadcasted_iota(jnp.int32, (SP_KC, cols), 0)
    _, acc_s = step(ks_scr[pl.ds(off, SP_KC), :], vs_scr[c_diag], k_pos <= q_pos, carry)

    def win(j, carry):
        off = pl.multiple_of(j * tq, tq)
        k_pos = off + lax.broadcasted_iota(jnp.int32, (tq, cols), 0)
        keep = (k_pos <= q_pos) & (k_pos > q_pos - WINDOW)
        return step(kw_scr[pl.ds(off, tq), :], vw_scr[j], keep, carry)

    _, acc_w = lax.fori_loop(jnp.maximum(i - WINDOW // tq, 0), i + 1, win, init)

    o_s = acc_s[:NSA_DH] / acc_s[NSA_DH:NSA_DH + 1]
    o_w = acc_w[:NSA_DH] / acc_w[NSA_DH:NSA_DH + 1]
    gates = _sigmoid(ng_ref[0])
    for h in range(NSA_HPG):
        fs = slice(h * NSA_DH, (h + 1) * NSA_DH)
        ts = slice(h * tq, (h + 1) * tq)
        r = h * N_BRANCH
        out_ref[0, fs, :] = (gates[r:r + 1] * oc_ref[0, fs, :]
                             + gates[r + 1:r + 2] * o_s[:, ts]
                             + gates[r + 2:r + 3] * o_w[:, ts])


def _sparse_attn(p, o_cmp_t, bias_t):
    nqt, ks, kw, vsw_t, ng_t = p["nqT"], p["ks"], p["kw"], p["vswT"], p["ngT"]
    b, _, t = nqt.shape
    n_slc = t // SLC_BLOCK
    gw = NSA_HPG * NSA_DH
    tq = SP_TQ
    qspec = pl.BlockSpec((1, gw, tq), lambda b_, g, i: (b_, g, i))
    kspec = pl.BlockSpec((1, t, NSA_KV), lambda b_, g, i: (b_, 0, 0))
    return pl.pallas_call(
        _sparse_kernel,
        grid=(b, NSA_G, t // tq),
        in_specs=[qspec,
                  pl.BlockSpec((1, 1, n_slc, tq), lambda b_, g, i: (b_, g, 0, i)),
                  kspec, kspec,
                  pl.BlockSpec((1, NSA_DH, t), lambda b_, g, i: (b_, g, 0)),
                  pl.BlockSpec((1, NSA_DH, t), lambda b_, g, i: (b_, g + NSA_G, 0)),
                  qspec,
                  pl.BlockSpec((1, GATE_ROWS, tq), lambda b_, g, i: (b_, g, i))],
        out_specs=qspec,
        out_shape=jax.ShapeDtypeStruct((b, NSA_WIDTH, t), F32),
        scratch_shapes=[pltpu.VMEM((t, AUG), BF16), pltpu.VMEM((t, AUG), BF16),
                        pltpu.VMEM((t // SP_KC, AUG, SP_KC), BF16),
                        pltpu.VMEM((t // tq, AUG, tq), BF16)],
        compiler_params=_params(("parallel", "parallel", "arbitrary")),
        name="sparse_attn",
    )(nqt, bias_t, ks, kw, vsw_t, vsw_t, o_cmp_t, ng_t)


def kernel(x, ffn1_norm, ffn1_w1, ffn1_w3, ffn1_w2, mix_norm, w_in, conv_w, ml_b_i, ml_b_f, ml_gn, cmp_k_pe, cmp_k_w1, cmp_k_b1, cmp_k_w2, cmp_v_pe, cmp_v_w1, cmp_v_b1, cmp_v_w2, w_out, ffn2_norm, ffn2_w1, ffn2_w3, ffn2_w2, final_norm):
    depth = ffn1_w1.shape[0]
    b, t, d = x.shape
    for l in range(depth):
        last = l == depth - 1
        x1, h = _ffn1(x.reshape(b * t, d), ffn1_norm[l].reshape(1, d), ffn1_w1[l].astype(BF16),
                      ffn1_w3[l].astype(BF16), ffn1_w2[l].astype(BF16), mix_norm[l].reshape(1, d))
        wt, wf = _in_proj_weights(w_in[l])
        p = _in_proj(h.reshape(b, t, d), wt, wf)

        bias = jnp.concatenate([ml_b_i[l], ml_b_f[l]])
        b_lane = jnp.pad(bias, (0, 128 - bias.shape[0])).reshape(1, 128)
        b_col = jnp.pad(bias, (0, GATE_ROWS - bias.shape[0])).reshape(GATE_ROWS, 1)
        hm = _mlstm(p, conv_w[l], b_lane, b_col, ml_gn[l].reshape(1, -1))

        kc, vct = _compress(p["kc"], p["vc"], cmp_k_pe[l], cmp_k_w1[l], cmp_k_b1[l], cmp_k_w2[l],
                            cmp_v_pe[l], cmp_v_w1[l], cmp_v_b1[l], cmp_v_w2[l])
        o_cmp_t, bias_t = _cmp_attn(p["nqT"], kc, vct)
        hn_t = _sparse_attn(p, o_cmp_t, bias_t)

        wo = w_out[l].astype(BF16)
        x = _out_ffn2(x1.reshape(b, t, d), hm, hn_t, wo[:ML_WIDTH], wo[ML_WIDTH:],
                      ffn2_norm[l].reshape(1, d), ffn2_w1[l].astype(BF16), ffn2_w3[l].astype(BF16),
                      ffn2_w2[l].astype(BF16), final_norm.reshape(1, d), final=last)
    return x
```

```python
import functools

import jax
import jax.numpy as jnp
import numpy as np
from jax import lax
from jax.experimental import pallas as pl
from jax.experimental.pallas import tpu as pltpu

F32 = jnp.float32
BF16 = jnp.bfloat16

EPS = 1e-6
NEG = -1e30
FORCE = 1e30
MASK = -(2.0 ** 100)

ML_HEADS = 4
ML_DH = 128
ML_WIDTH = ML_HEADS * ML_DH
ML_CHUNK = 256
CONV_K = 4
NSA_G = 2
NSA_HPG = 4
NSA_DH = 64
NSA_WIDTH = NSA_G * NSA_HPG * NSA_DH
NSA_KV = NSA_G * NSA_DH
CMP_BLOCK = 32
CMP_STRIDE = 16
SLC_BLOCK = 64
SLC_TOPN = 16
WINDOW = 512
N_BRANCH = 3
GATE_ROWS = 16

V7X_VMEM_BYTES = 64 * 1024 * 1024
VMEM_LIMIT = 56 * 1024 * 1024

NT = (((1,), (1,)), ((), ()))


def _dot(a, b):
    return jnp.dot(a, b, preferred_element_type=F32)


def _dot_nt(a, b):
    return lax.dot_general(a, b, NT, preferred_element_type=F32)


def _sigmoid(x):
    return 1.0 / (1.0 + jnp.exp(-x))


def _silu(x):
    return x * _sigmoid(x)


def _log_sigmoid(x):
    return -(jnp.maximum(-x, 0.0) + jnp.log(1.0 + jnp.exp(-jnp.abs(x))))


def _rms(x, g):
    return x * lax.rsqrt(jnp.mean(x * x, axis=-1, keepdims=True) + EPS) * g


def _split3(x):
    hi = x.astype(BF16)
    r = x - hi.astype(F32)
    mid = r.astype(BF16)
    lo = (r - mid.astype(F32)).astype(BF16)
    return hi, mid, lo


def _resident(shape):
    nd = len(shape)
    return pl.BlockSpec(shape, lambda *_: (0,) * nd, pipeline_mode=pl.Buffered(1))


def _params(sem):
    return pltpu.CompilerParams(dimension_semantics=sem, vmem_limit_bytes=VMEM_LIMIT)


FF_CHUNK = 256


def _swiglu(h, w1_ref, w3_ref, w2_ref):
    d_ff = w1_ref.shape[1]
    acc = None
    for c in range(d_ff // FF_CHUNK):
        sl = slice(c * FF_CHUNK, (c + 1) * FF_CHUNK)
        a = _dot(h, w1_ref[:, sl])
        b = _dot(h, w3_ref[:, sl])
        g = (_silu(a) * b).astype(BF16)
        y = _dot(g, w2_ref[sl, :])
        acc = y if acc is None else acc + y
    return acc


def _ffn1_kernel(x_ref, g_ref, w1_ref, w3_ref, w2_ref, gm_ref, x1_ref, h_ref):
    x = x_ref[...]
    h = _rms(x, g_ref[...]).astype(BF16)
    x1 = x + 0.5 * _swiglu(h, w1_ref, w3_ref, w2_ref)
    x1_ref[...] = x1
    h_ref[...] = _rms(x1, gm_ref[...]).astype(BF16)


def _ffn1(x2d, g1, w1, w3, w2, gmix, tm=512):
    n, d = x2d.shape
    f = w1.shape[1]
    row = pl.BlockSpec((tm, d), lambda i: (i, 0))
    return pl.pallas_call(
        _ffn1_kernel,
        grid=(n // tm,),
        in_specs=[row, _resident((1, d)), _resident((d, f)), _resident((d, f)),
                  _resident((f, d)), _resident((1, d))],
        out_specs=[row, row],
        out_shape=[jax.ShapeDtypeStruct((n, d), F32), jax.ShapeDtypeStruct((n, d), BF16)],
        compiler_params=_params(("parallel",)),
        name="ffn1",
    )(x2d, g1, w1, w3, w2, gmix)


def _out_ffn2_kernel(x1_ref, hm_ref, hnt_ref, wom_ref, won_ref, g_ref, w1_ref, w3_ref, w2_ref,
                     gf_ref, o_ref, *, final):
    hm = hm_ref[0].astype(BF16)
    hn = hnt_ref[0].T.astype(BF16)
    x2 = x1_ref[0] + _dot(hm, wom_ref[...]) + _dot(hn, won_ref[...])
    h = _rms(x2, g_ref[...]).astype(BF16)
    x3 = x2 + 0.5 * _swiglu(h, w1_ref, w3_ref, w2_ref)
    o_ref[0] = _rms(x3, gf_ref[...]) if final else x3


def _out_ffn2(x1, hm, hnt, wo_m, wo_n, g2, w1, w3, w2, gfin, final, tm=512):
    b, t, d = x1.shape
    f = w1.shape[1]
    wm = hm.shape[2]
    wn = hnt.shape[1]
    return pl.pallas_call(
        functools.partial(_out_ffn2_kernel, final=final),
        grid=(b, t // tm),
        in_specs=[pl.BlockSpec((1, tm, d), lambda i, j: (i, j, 0)),
                  pl.BlockSpec((1, tm, wm), lambda i, j: (i, j, 0)),
                  pl.BlockSpec((1, wn, tm), lambda i, j: (i, 0, j)),
                  _resident((wm, d)), _resident((wn, d)), _resident((1, d)),
                  _resident((d, f)), _resident((d, f)), _resident((f, d)), _resident((1, d))],
        out_specs=pl.BlockSpec((1, tm, d), lambda i, j: (i, j, 0)),
        out_shape=jax.ShapeDtypeStruct((b, t, d), F32),
        compiler_params=_params(("parallel", "parallel")),
        name="out_ffn2",
    )(x1, hm, hnt, wo_m, wo_n, g2, w1, w3, w2, gfin)


TOK_SLABS = (("qk", 2 * ML_WIDTH), ("v", ML_WIDTH), ("o", ML_WIDTH), ("kc", NSA_KV),
             ("vc", NSA_KV), ("ks", NSA_KV), ("kw", NSA_KV), ("g", 128))
FM_SLABS = (("nq", NSA_WIDTH), ("vsw", 2 * NSA_KV), ("mv", ML_WIDTH), ("mg", GATE_ROWS),
            ("ng", NSA_G * GATE_ROWS))


def _in_proj_kernel(h_ref, wt_ref, wf_ref, *out_refs):
    h = h_ref[0]
    n_tok = len(TOK_SLABS)
    off = 0
    for (_, w), ref in zip(TOK_SLABS, out_refs[:n_tok]):
        ref[0] = _dot(h, wt_ref[:, off:off + w])
        off += w
    off = 0
    for (_, r), ref in zip(FM_SLABS, out_refs[n_tok:]):
        ref[0] = _dot_nt(wf_ref[off:off + r, :], h)
        off += r


def _in_proj(h3, wt, wf, tm=512):
    b, t, d = h3.shape
    out_specs, out_shape = [], []
    for _, w in TOK_SLABS:
        out_specs.append(pl.BlockSpec((1, tm, w), lambda i, j: (i, j, 0)))
        out_shape.append(jax.ShapeDtypeStruct((b, t, w), F32))
    for _, r in FM_SLABS:
        out_specs.append(pl.BlockSpec((1, r, tm), lambda i, j: (i, 0, j)))
        out_shape.append(jax.ShapeDtypeStruct((b, r, t), F32))
    outs = pl.pallas_call(
        _in_proj_kernel,
        grid=(b, t // tm),
        in_specs=[pl.BlockSpec((1, tm, d), lambda i, j: (i, j, 0)),
                  _resident(wt.shape), _resident(wf.shape)],
        out_specs=out_specs,
        out_shape=out_shape,
        compiler_params=_params(("parallel", "parallel")),
        name="in_proj",
    )(h3, wt, wf)
    names = [n for n, _ in TOK_SLABS] + [n + "T" for n, _ in FM_SLABS]
    return dict(zip(names, outs))


def _in_proj_weights(w_in):
    sizes = [ML_WIDTH] * 4 + [ML_HEADS] * 2 + [NSA_WIDTH] + [NSA_KV] * 6 + [NSA_G * NSA_HPG * N_BRANCH]
    offs = np.cumsum([0] + sizes)
    mq, mk, mv, mo, mi, mf, nq, kc, vc, ks, vs, kw, vw, ng = (
        w_in[:, offs[i]:offs[i + 1]] for i in range(len(sizes)))
    d = w_in.shape[0]
    gpad = jnp.zeros((d, 128 - 2 * ML_HEADS), w_in.dtype)
    wt = jnp.concatenate([mq, mk, mv, mo, kc, vc, ks, kw, mi, mf, gpad], axis=1)
    per_g = NSA_HPG * N_BRANCH
    zg = jnp.zeros((d, GATE_ROWS - per_g), w_in.dtype)
    ng_rows = jnp.concatenate([ng[:, :per_g], zg, ng[:, per_g:], zg], axis=1)
    mg_rows = jnp.concatenate([mi, mf, jnp.zeros((d, GATE_ROWS - 2 * ML_HEADS), w_in.dtype)], axis=1)
    wf = jnp.concatenate([nq, vs, vw, mv, mg_rows, ng_rows], axis=1).T
    return wt.astype(BF16), wf.astype(BF16)


def _causal_conv_silu(x, w):
    head = x[:8]
    rows = lax.broadcasted_iota(jnp.int32, head.shape, 0)
    y = x * w[CONV_K - 1:CONV_K, :]
    for j in range(CONV_K - 1):
        s = CONV_K - 1 - j
        xs = pltpu.roll(x, s, axis=0)
        xs = jnp.concatenate([jnp.where(rows >= s, xs[:8], 0.0), xs[8:]], axis=0)
        y = y + xs * w[j:j + 1, :]
    return _silu(y)


def _mlstm_kernel(q_ref, k_ref, v_ref, vt_ref, o_ref, gtok_ref, grow_ref, wq_ref, wk_ref,
                  blane_ref, bcol_ref, gn_ref, out_ref, q_scr, k_scr, gt_scr, gr_scr):
    L = ML_CHUNK
    hd = pl.program_id(1)
    t = q_ref.shape[1]
    q_scr[...] = _causal_conv_silu(q_ref[0], wq_ref[...]) * (ML_DH ** -0.5)
    k_scr[...] = _causal_conv_silu(k_ref[0], wk_ref[...])

    r_i = lax.broadcasted_iota(jnp.int32, (L, L), 0)
    c_i = lax.broadcasted_iota(jnp.int32, (L, L), 1)
    causal = r_i >= c_i
    lane = lax.broadcasted_iota(jnp.int32, (L, 128), 1)
    srow = lax.broadcasted_iota(jnp.int32, (GATE_ROWS, L), 0)

    @pl.when(hd == 0)
    def _():
        tril = jnp.where(causal, 1.0, 0.0).astype(BF16)
        triu = jnp.where(r_i <= c_i, 1.0, 0.0).astype(BF16)
        f_lane = (lane >= ML_HEADS) & (lane < 2 * ML_HEADS)
        f_row = (srow >= ML_HEADS) & (srow < 2 * ML_HEADS)
        for c in range(t // L):
            sl = slice(c * L, (c + 1) * L)
            gt = gtok_ref[0, sl, :] + blane_ref[...]
            hi, mid, lo = _split3(jnp.where(f_lane, _log_sigmoid(gt), 0.0))
            cs_t = _dot(tril, hi) + _dot(tril, mid) + _dot(tril, lo)
            gt_scr[sl, :] = jnp.where(f_lane, cs_t, gt)
            gr = grow_ref[0, :, sl] + bcol_ref[...]
            hi, mid, lo = _split3(jnp.where(f_row, _log_sigmoid(gr), 0.0))
            cs_r = _dot(hi, triu) + _dot(mid, triu) + _dot(lo, triu)
            gr_scr[:, sl] = jnp.where(f_row, cs_r, gr)

    gn = gn_ref[...]
    ones_l = jnp.ones((L, ML_DH), BF16)
    state = jnp.zeros((2 * ML_DH, ML_DH), F32)
    m_state = jnp.zeros((1, 1), F32)

    for c in range(t // L):
        sl = slice(c * L, (c + 1) * L)
        qb = q_scr[sl, :].astype(BF16)
        kb = k_scr[sl, :].astype(BF16)
        b_col = jnp.sum(jnp.where(lane == hd + ML_HEADS, gt_scr[sl, :], 0.0), axis=1, keepdims=True)
        gr = gr_scr[:, sl]
        b_row = jnp.sum(jnp.where(srow == hd + ML_HEADS, gr, 0.0), axis=0, keepdims=True)
        i_row = jnp.sum(jnp.where(srow == hd, gr, 0.0), axis=0, keepdims=True)
        g_tot = b_row[:, L - 1:L]

        d_log = b_col - b_row + i_row
        m_intra = jnp.max(jnp.where(causal, d_log, -jnp.inf), axis=1, keepdims=True)
        m_inter = b_col + m_state
        m_t = jnp.maximum(m_inter, m_intra)
        decay = jnp.where(causal, jnp.exp(d_log - m_t), 0.0)
        s_mat = (_dot_nt(qb, kb) * decay).astype(BF16)
        r = jnp.exp(m_inter - m_t)
        v_aug = jnp.concatenate([v_ref[0, sl, :].astype(BF16), ones_l], axis=1)
        both = _dot(s_mat, v_aug) + r * _dot_nt(qb, state.astype(BF16))
        num, den = both[:, :ML_DH], both[:, ML_DH:]
        hh = num / jnp.maximum(jnp.abs(den), jnp.exp(-m_t))
        y = hh * lax.rsqrt(jnp.mean(hh * hh, axis=1, keepdims=True) + EPS) * gn
        out_ref[0, sl, :] = y * _sigmoid(o_ref[0, sl, :])

        w_row = g_tot - b_row + i_row
        m_loc = jnp.max(w_row, axis=1, keepdims=True)
        m_new = jnp.maximum(g_tot + m_state, m_loc)
        a = jnp.exp(g_tot + m_state - m_new)
        e_row = jnp.exp(w_row - m_new)
        upd = jnp.concatenate([vt_ref[0, :, sl] * e_row, jnp.broadcast_to(e_row, (ML_DH, L))], axis=0)
        state = a * state + _dot(upd.astype(BF16), kb)
        m_state = m_new


def _mlstm(p, conv_w, b_lane, b_col, gn):
    qk, v, vt, o, gtok, grow = p["qk"], p["v"], p["mvT"], p["o"], p["g"], p["mgT"]
    b, t, _ = qk.shape
    hw = ML_DH
    tok = lambda off: pl.BlockSpec((1, t, hw), lambda i, h: (i, 0, h + off))
    return pl.pallas_call(
        _mlstm_kernel,
        grid=(b, ML_HEADS),
        in_specs=[tok(0), tok(ML_HEADS), tok(0),
                  pl.BlockSpec((1, hw, t), lambda i, h: (i, h, 0)),
                  tok(0),
                  pl.BlockSpec((1, t, 128), lambda i, h: (i, 0, 0)),
                  pl.BlockSpec((1, GATE_ROWS, t), lambda i, h: (i, 0, 0)),
                  pl.BlockSpec((CONV_K, hw), lambda i, h: (0, h)),
                  pl.BlockSpec((CONV_K, hw), lambda i, h: (0, h + ML_HEADS)),
                  pl.BlockSpec((1, 128), lambda i, h: (0, 0)),
                  pl.BlockSpec((GATE_ROWS, 1), lambda i, h: (0, 0)),
                  pl.BlockSpec((1, hw), lambda i, h: (0, h))],
        out_specs=tok(0),
        out_shape=jax.ShapeDtypeStruct((b, t, ML_WIDTH), F32),
        scratch_shapes=[pltpu.VMEM((t, hw), F32), pltpu.VMEM((t, hw), F32),
                        pltpu.VMEM((t, 128), F32), pltpu.VMEM((GATE_ROWS, t), F32)],
        compiler_params=_params(("parallel", "arbitrary")),
        name="mlstm",
    )(qk, qk, v, vt, o, gtok, grow, conv_w, conv_w, b_lane, b_col, gn)


def _compress_kernel(xk_ref, xv_ref, kw0_ref, kw1_ref, kpe_ref, kw1f_ref, kb1_ref, kw2_ref,
                     vw0_ref, vw1_ref, vpe_ref, vw1f_ref, vb1_ref, vw2t_ref, kc_ref, vct_ref):
    hid = kw2_ref.shape[0]

    def hidden(x_ref, w0_ref, w1_ref, pe_ref, w1f_ref, b1_ref):
        x = x_ref[0].astype(BF16)
        n = x.shape[0]
        p0 = _dot(x, w0_ref[...])
        p1 = _dot(x, w1_ref[...])
        pe = jnp.broadcast_to(pe_ref[...], (8, pe_ref.shape[1])).astype(BF16)
        cvec = _dot(pe, w1f_ref[...])[0:1, :] + b1_ref[...]
        cvec = jnp.concatenate([cvec] * NSA_G, axis=1)
        return _silu(p0 + pltpu.roll(p1, n - 1, axis=0) + cvec)

    hk = hidden(xk_ref, kw0_ref, kw1_ref, kpe_ref, kw1f_ref, kb1_ref)
    hv = hidden(xv_ref, vw0_ref, vw1_ref, vpe_ref, vw1f_ref, vb1_ref)
    for g in range(NSA_G):
        sl = slice(g * hid, (g + 1) * hid)
        kc_ref[0, g] = _dot(hk[:, sl].astype(BF16), kw2_ref[...])
        vct_ref[0, g] = _dot_nt(vw2t_ref[...], hv[:, sl].astype(BF16))


def _compress_weights(w1):
    hid = w1.shape[1]
    w = w1.reshape(2, CMP_STRIDE, NSA_DH, hid)
    z = jnp.zeros_like(w)
    g0 = jnp.concatenate([w, z], axis=-1)
    g1 = jnp.concatenate([z, w], axis=-1)
    big = jnp.stack([g0, g1], axis=2)
    big = big.reshape(2, CMP_STRIDE * NSA_G * NSA_DH, 2 * hid).astype(BF16)
    return big[0], big[1]


def _compress(kc_tok, vc_tok, k_pe, k_w1, k_b1, k_w2, v_pe, v_w1, v_b1, v_w2):
    b, t, _ = kc_tok.shape
    n_slab = t // CMP_STRIDE
    hid = k_w1.shape[1]
    xk = kc_tok.reshape(b, n_slab, CMP_STRIDE * NSA_KV)
    xv = vc_tok.reshape(b, n_slab, CMP_STRIDE * NSA_KV)
    kw0, kw1 = _compress_weights(k_w1)
    vw0, vw1 = _compress_weights(v_w1)
    xspec = pl.BlockSpec((1, n_slab, CMP_STRIDE * NSA_KV), lambda i: (i, 0, 0))
    args = (xk, xv,
            kw0, kw1, k_pe.reshape(1, -1), k_w1.astype(BF16), k_b1.reshape(1, -1), k_w2.astype(BF16),
            vw0, vw1, v_pe.reshape(1, -1), v_w1.astype(BF16), v_b1.reshape(1, -1),
            v_w2.T.astype(BF16))
    return pl.pallas_call(
        _compress_kernel,
        grid=(b,),
        in_specs=[xspec, xspec] + [_resident(a.shape) for a in args[2:]],
        out_specs=[pl.BlockSpec((1, NSA_G, n_slab, NSA_DH), lambda i: (i, 0, 0, 0)),
                   pl.BlockSpec((1, NSA_G, NSA_DH, n_slab), lambda i: (i, 0, 0, 0))],
        out_shape=[jax.ShapeDtypeStruct((b, NSA_G, n_slab, NSA_DH), F32),
                   jax.ShapeDtypeStruct((b, NSA_G, NSA_DH, n_slab), F32)],
        compiler_params=_params(("parallel",)),
        name="compress",
    )(*args)


def _cmp_attn_kernel(qt_ref, kc_ref, vct_ref, o_ref, bias_ref):
    tq = qt_ref.shape[2]
    n_cmp = kc_ref.shape[2]
    n_slc = bias_ref.shape[2]
    t0 = pl.program_id(2) * tq
    kcm = kc_ref[0, 0].astype(BF16)
    vct = vct_ref[0, 0].astype(BF16)
    n_i = lax.broadcasted_iota(jnp.int32, (n_cmp, tq), 0)
    t_i = lax.broadcasted_iota(jnp.int32, (n_cmp, tq), 1) + t0
    valid = n_i * CMP_STRIDE + (CMP_BLOCK - 1) <= t_i
    psum = jnp.zeros((n_cmp, tq), F32)
    for h in range(NSA_HPG):
        sl = slice(h * NSA_DH, (h + 1) * NSA_DH)
        qh = (qt_ref[0, sl, :] * (NSA_DH ** -0.5)).astype(BF16)
        s = jnp.where(valid, _dot(kcm, qh), NEG)
        m = jnp.max(s, axis=0, keepdims=True)
        e = jnp.exp(s - m)
        p = jnp.where(valid, e * (1.0 / jnp.sum(e, axis=0, keepdims=True)), 0.0)
        o_ref[0, sl, :] = _dot(vct, p.astype(BF16))
        psum = psum + p

    j_o = lax.broadcasted_iota(jnp.int32, (n_slc, n_cmp), 0) * SLC_BLOCK
    c_o = lax.broadcasted_iota(jnp.int32, (n_slc, n_cmp), 1) * CMP_STRIDE
    ov = jnp.maximum(jnp.minimum(c_o + CMP_BLOCK, j_o + SLC_BLOCK) - jnp.maximum(c_o, j_o), 0)
    ov = (ov.astype(F32) * (1.0 / CMP_BLOCK)).astype(BF16)
    hi, mid, lo = _split3(psum)
    imp = _dot(ov, hi) + _dot(ov, mid) + _dot(ov, lo)

    j_i = lax.broadcasted_iota(jnp.int32, (n_slc, tq), 0)
    tt = lax.broadcasted_iota(jnp.int32, (n_slc, tq), 1) + t0
    cur = lax.shift_right_logical(tt, int(np.log2(SLC_BLOCK)))
    forced = (j_i == 0) | (j_i == cur) | (j_i == cur - 1)
    score = jnp.where(forced, FORCE, jnp.where(j_i * SLC_BLOCK <= tt, imp, NEG))
    rank = jnp.zeros((n_slc, tq), F32)
    for i in range(n_slc):
        row = score[i:i + 1, :]
        beats = (row > score) | ((row == score) & (j_i > i))
        rank = rank + jnp.where(beats, 1.0, 0.0)
    n_sel = min(SLC_TOPN, n_slc)
    bias_ref[0, 0] = jnp.where(rank < n_sel, 0.0, MASK).astype(BF16)


def _cmp_attn(nqt, kc, vct, tq=512):
    b, _, t = nqt.shape
    n_cmp = kc.shape[2]
    n_slc = t // SLC_BLOCK
    gw = NSA_HPG * NSA_DH
    return pl.pallas_call(
        _cmp_attn_kernel,
        grid=(b, NSA_G, t // tq),
        in_specs=[pl.BlockSpec((1, gw, tq), lambda i, g, j: (i, g, j)),
                  pl.BlockSpec((1, 1, n_cmp, NSA_DH), lambda i, g, j: (i, g, 0, 0)),
                  pl.BlockSpec((1, 1, NSA_DH, n_cmp), lambda i, g, j: (i, g, 0, 0))],
        out_specs=[pl.BlockSpec((1, gw, tq), lambda i, g, j: (i, g, j)),
                   pl.BlockSpec((1, 1, n_slc, tq), lambda i, g, j: (i, g, 0, j))],
        out_shape=[jax.ShapeDtypeStruct((b, NSA_WIDTH, t), F32),
                   jax.ShapeDtypeStruct((b, NSA_G, n_slc, t), BF16)],
        compiler_params=_params(("parallel", "parallel", "parallel")),
        name="cmp_attn",
    )(nqt, kc, vct)


SP_TQ = 256
SP_KC = SP_TQ
AUG = 128
V_ROWS = NSA_DH + 16
LOG2E = 1.4426950408889634
assert WINDOW == 2 * SP_KC


def _sparse_kernel(qt_ref, bias_ref, ks_ref, kw_ref, vst_ref, vwt_ref, oc_ref, ng_ref, out_ref,
                   ks_scr, kw_scr, vs_scr, vw_scr, ss_scr, sw_scr):
    g = pl.program_id(1)
    i = pl.program_id(2)
    t = ks_ref.shape[1]
    n_slc = bias_ref.shape[2]
    tq = SP_TQ
    kc = SP_KC
    cols = NSA_HPG * tq

    @pl.when(i == 0)
    def _():
        lane = lax.broadcasted_iota(jnp.int32, (t, AUG), 1)
        key = lax.broadcasted_iota(jnp.int32, (t, AUG), 0)
        ind = jnp.where(lane - NSA_DH == lax.shift_right_logical(key, int(np.log2(SLC_BLOCK))),
                        1.0, 0.0)

        def group_lanes(x):
            return jnp.where(g == 0, x, pltpu.roll(x, NSA_DH, axis=1))

        ks_scr[...] = jnp.where(lane < NSA_DH, group_lanes(ks_ref[0]), ind).astype(BF16)
        kw_scr[...] = jnp.where(lane < NSA_DH, group_lanes(kw_ref[0]), 0.0).astype(BF16)
        ones_row = jnp.where(lax.broadcasted_iota(jnp.int32, (V_ROWS - NSA_DH, kc), 0) == 0, 1.0, 0.0)
        for c in range(t // kc):
            sl = slice(c * kc, (c + 1) * kc)
            vs_scr[c] = jnp.concatenate([vst_ref[0, :, sl], ones_row], axis=0).astype(BF16)
            vw_scr[c] = jnp.concatenate([vwt_ref[0, :, sl], ones_row], axis=0).astype(BF16)

    q_scale = (NSA_DH ** -0.5) * LOG2E
    q_rows = jnp.concatenate(
        [(qt_ref[0, h * NSA_DH:(h + 1) * NSA_DH, :] * q_scale).astype(BF16)
         for h in range(NSA_HPG)], axis=1)
    b_rows = jnp.concatenate([bias_ref[0, 0]] * NSA_HPG, axis=1)
    z_rows = jnp.zeros((AUG - NSA_DH - n_slc, cols), BF16)
    q_aug = jnp.concatenate([q_rows, b_rows, z_rows], axis=0)

    r_i = lax.broadcasted_iota(jnp.int32, (kc, cols), 0)
    c_i = lax.broadcasted_iota(jnp.int32, (kc, cols), 1) & (tq - 1)
    tri = r_i <= c_i

    def scores(k_scr, c, keep, s_ref):
        off = pl.multiple_of(c * kc, kc)
        s = _dot(k_scr[pl.ds(off, kc), :], q_aug)
        if keep is not None:
            s = jnp.where(keep, s, MASK)
        s_ref[...] = s
        return jnp.max(s, axis=0, keepdims=True)

    def absorb(v_scr, c, s_ref, m_s, m_run, acc):
        m_new = jnp.maximum(m_run, m_s)
        pv = _dot(v_scr[c], jnp.exp2(s_ref[...] - m_new).astype(BF16))
        return m_new, jnp.exp2(m_run - m_new) * acc + pv

    m0 = jnp.full((1, cols), MASK, F32)
    acc0 = jnp.zeros((V_ROWS, cols), F32)

    sa, sb = ss_scr.at[0], ss_scr.at[1]

    def sel_body(j, carry):
        m_a, m_run, acc = carry
        c = 2 * j
        m_b = scores(ks_scr, c + 1, None, sb)
        m_run, acc = absorb(vs_scr, c, sa, m_a, m_run, acc)
        m_a = scores(ks_scr, c + 2, None, sa)
        m_run, acc = absorb(vs_scr, c + 1, sb, m_b, m_run, acc)
        return m_a, m_run, acc

    m_a = scores(ks_scr, 0, jnp.logical_or(tri, i > 0), sa)
    m_a, m_run, acc = lax.fori_loop(0, jnp.maximum(i - 1, 0) // 2, sel_body, (m_a, m0, acc0))

    def odd_tail():
        m_b = scores(ks_scr, i, tri, sb)
        m_r, a = absorb(vs_scr, i - 1, sa, m_a, m_run, acc)
        return absorb(vs_scr, i, sb, m_b, m_r, a)

    def even_tail():
        m_b = scores(ks_scr, i - 1, None, sb)
        m_r, a = absorb(vs_scr, i - 2, sa, m_a, m_run, acc)
        m_d = scores(ks_scr, i, tri, sa)
        m_r, a = absorb(vs_scr, i - 1, sb, m_b, m_r, a)
        return absorb(vs_scr, i, sa, m_d, m_r, a)

    first_tail = lambda: absorb(vs_scr, 0, sa, m_a, m_run, acc)
    _, acc_s = lax.cond(i % 2 == 1, odd_tail, lambda: lax.cond(i == 0, first_tail, even_tail))

    win = ((jnp.maximum(i - 2, 0), jnp.logical_and(jnp.logical_not(tri), i >= 2)),
           (jnp.maximum(i - 1, 0), i >= 1),
           (i, tri))
    m_w = [scores(kw_scr, c, keep, sw_scr.at[n]) for n, (c, keep) in enumerate(win)]
    m_run, acc_w = m0, acc0
    for n, (c, _) in enumerate(win):
        m_run, acc_w = absorb(vw_scr, c, sw_scr.at[n], m_w[n], m_run, acc_w)

    o_s = acc_s[:NSA_DH] / acc_s[NSA_DH:NSA_DH + 1]
    o_w = acc_w[:NSA_DH] / acc_w[NSA_DH:NSA_DH + 1]
    gates = _sigmoid(ng_ref[0])
    for h in range(NSA_HPG):
        fs = slice(h * NSA_DH, (h + 1) * NSA_DH)
        ts = slice(h * tq, (h + 1) * tq)
        r = h * N_BRANCH
        out_ref[0, fs, :] = (gates[r:r + 1] * oc_ref[0, fs, :]
                             + gates[r + 1:r + 2] * o_s[:, ts]
                             + gates[r + 2:r + 3] * o_w[:, ts])


def _sparse_attn(p, o_cmp_t, bias_t):
    nqt, ks, kw, vsw_t, ng_t = p["nqT"], p["ks"], p["kw"], p["vswT"], p["ngT"]
    b, _, t = nqt.shape
    n_slc = t // SLC_BLOCK
    gw = NSA_HPG * NSA_DH
    tq = SP_TQ
    qspec = pl.BlockSpec((1, gw, tq), lambda b_, g, i: (b_, g, i))
    kspec = pl.BlockSpec((1, t, NSA_KV), lambda b_, g, i: (b_, 0, 0))
    return pl.pallas_call(
        _sparse_kernel,
        grid=(b, NSA_G, t // tq),
        in_specs=[qspec,
                  pl.BlockSpec((1, 1, n_slc, tq), lambda b_, g, i: (b_, g, 0, i)),
                  kspec, kspec,
                  pl.BlockSpec((1, NSA_DH, t), lambda b_, g, i: (b_, g, 0)),
                  pl.BlockSpec((1, NSA_DH, t), lambda b_, g, i: (b_, g + NSA_G, 0)),
                  qspec,
                  pl.BlockSpec((1, GATE_ROWS, tq), lambda b_, g, i: (b_, g, i))],
        out_specs=qspec,
        out_shape=jax.ShapeDtypeStruct((b, NSA_WIDTH, t), F32),
        scratch_shapes=[pltpu.VMEM((t, AUG), BF16), pltpu.VMEM((t, AUG), BF16),
                        pltpu.VMEM((t // SP_KC, V_ROWS, SP_KC), BF16),
                        pltpu.VMEM((t // SP_KC, V_ROWS, SP_KC), BF16),
                        pltpu.VMEM((2, SP_KC, NSA_HPG * tq), F32),
                        pltpu.VMEM((3, SP_KC, NSA_HPG * tq), F32)],
        compiler_params=_params(("parallel", "parallel", "arbitrary")),
        name="sparse_attn",
    )(nqt, bias_t, ks, kw, vsw_t, vsw_t, o_cmp_t, ng_t)


def kernel(x, ffn1_norm, ffn1_w1, ffn1_w3, ffn1_w2, mix_norm, w_in, conv_w, ml_b_i, ml_b_f, ml_gn, cmp_k_pe, cmp_k_w1, cmp_k_b1, cmp_k_w2, cmp_v_pe, cmp_v_w1, cmp_v_b1, cmp_v_w2, w_out, ffn2_norm, ffn2_w1, ffn2_w3, ffn2_w2, final_norm):
    depth = ffn1_w1.shape[0]
    b, t, d = x.shape
    for l in range(depth):
        last = l == depth - 1
        x1, h = _ffn1(x.reshape(b * t, d), ffn1_norm[l].reshape(1, d), ffn1_w1[l].astype(BF16),
                      ffn1_w3[l].astype(BF16), ffn1_w2[l].astype(BF16), mix_norm[l].reshape(1, d))
        wt, wf = _in_proj_weights(w_in[l])
        p = _in_proj(h.reshape(b, t, d), wt, wf)

        bias = jnp.concatenate([ml_b_i[l], ml_b_f[l]])
        b_lane = jnp.pad(bias, (0, 128 - bias.shape[0])).reshape(1, 128)
        b_col = jnp.pad(bias, (0, GATE_ROWS - bias.shape[0])).reshape(GATE_ROWS, 1)
        hm = _mlstm(p, conv_w[l], b_lane, b_col, ml_gn[l].reshape(1, -1))

        kc, vct = _compress(p["kc"], p["vc"], cmp_k_pe[l], cmp_k_w1[l], cmp_k_b1[l], cmp_k_w2[l],
                            cmp_v_pe[l], cmp_v_w1[l], cmp_v_b1[l], cmp_v_w2[l])
        o_cmp_t, bias_t = _cmp_attn(p["nqT"], kc, vct)
        hn_t = _sparse_attn(p, o_cmp_t, bias_t)

        wo = w_out[l].astype(BF16)
        x = _out_ffn2(x1.reshape(b, t, d), hm, hn_t, wo[:ML_WIDTH], wo[ML_WIDTH:],
                      ffn2_norm[l].reshape(1, d), ffn2_w1[l].astype(BF16), ffn2_w3[l].astype(BF16),
                      ffn2_w2[l].astype(BF16), final_norm.reshape(1, d), final=last)
    return x
```
